```python
import jax, jax.numpy as jnp
from jax import lax
import numpy as np

D_MODEL = 2048
BATCH = 4
SEQ = 2048
DEPTH = 4

N_HEADS = 8
HEAD_DIM = 128
ATTN_WIDTH = N_HEADS * HEAD_DIM
CONV_WIDTH = D_MODEL // 2
CONV_KERNEL = 31
D_FF = 5632
N_EXPERTS = 8
TOP_K = 2
Q_BLOCK = 128
LN_EPS = 1e-5
DEEPNORM_ALPHA = (2 * DEPTH) ** 0.25
DEEPNORM_BETA = (8 * DEPTH) ** -0.25
FORGET_BIAS_SHIFT = 3.0
N_DENSE = (DEPTH + 1) // 2
N_MOE = DEPTH // 2

_OFF_Q = 2 * CONV_WIDTH
_OFF_K = _OFF_Q + ATTN_WIDTH
_OFF_V = _OFF_K + ATTN_WIDTH
_OFF_F = _OFF_V + ATTN_WIDTH
_OFF_G = _OFF_F + N_HEADS
IN_COLS = _OFF_G + 2 * D_MODEL

kernel_name = "hybrid_conformer_fox_moe_deepnorm"


def layer_norm(x, g, b):
    xf = x.astype(jnp.float32)
    mu = jnp.mean(xf, axis=-1, keepdims=True)
    var = jnp.mean(jnp.square(xf - mu), axis=-1, keepdims=True)
    y = (xf - mu) * lax.rsqrt(var + LN_EPS)
    return (y * g.astype(jnp.float32) + b.astype(jnp.float32)).astype(x.dtype)


def conv_branch(u_glu, dw_w, dw_b, ln_g, ln_b, w_out, b_out):
    a, gate = jnp.split(u_glu, 2, axis=-1)
    u = a * jax.nn.sigmoid(gate)
    y = lax.conv_general_dilated(
        u, dw_w[:, None, :], window_strides=(1,),
        padding=[(CONV_KERNEL - 1, 0)],
        dimension_numbers=("NWC", "WIO", "NWC"),
        feature_group_count=CONV_WIDTH) + dw_b
    y = jax.nn.silu(layer_norm(y, ln_g, ln_b))
    return y @ w_out + b_out


def forgetting_attention(q, k, v, f_logit):
    B, S = q.shape[0], q.shape[1]
    nb = S // Q_BLOCK
    cum = jnp.cumsum(jax.nn.log_sigmoid(f_logit.astype(jnp.float32)), axis=1)
    cum = jnp.transpose(cum, (0, 2, 1))
    qh = jnp.transpose(q, (0, 2, 1, 3)) * (HEAD_DIM ** -0.5)
    kh = jnp.transpose(k, (0, 2, 1, 3))
    vh = jnp.transpose(v, (0, 2, 1, 3))
    q_blocks = qh.reshape(B, N_HEADS, nb, Q_BLOCK, HEAD_DIM).transpose(2, 0, 1, 3, 4)
    c_blocks = cum.reshape(B, N_HEADS, nb, Q_BLOCK).transpose(2, 0, 1, 3)
    k_pos = jnp.arange(S)

    def one_block(args):
        qb, cb, i = args
        s = jnp.einsum("bhqd,bhkd->bhqk", qb, kh).astype(jnp.float32)
        s = s + cb[..., None] - cum[:, :, None, :]
        q_pos = i * Q_BLOCK + jnp.arange(Q_BLOCK)
        s = jnp.where(k_pos[None, :] <= q_pos[:, None], s, -jnp.inf)
        p = jax.nn.softmax(s, axis=-1).astype(vh.dtype)
        return jnp.einsum("bhqk,bhkd->bhqd", p, vh)

    out = lax.map(one_block, (q_blocks, c_blocks, jnp.arange(nb)))
    return out.transpose(1, 0, 3, 2, 4).reshape(B, S, ATTN_WIDTH)


def mixer(x, w_in, b_in, conv_w, conv_b, conv_ln_g, conv_ln_b, w_conv_out, b_conv_out,
          w_attn_out, w_o):
    B, S, _ = x.shape
    p = x @ w_in + b_in
    q = p[..., _OFF_Q:_OFF_K].reshape(B, S, N_HEADS, HEAD_DIM)
    k = p[..., _OFF_K:_OFF_V].reshape(B, S, N_HEADS, HEAD_DIM)
    v = p[..., _OFF_V:_OFF_F].reshape(B, S, N_HEADS, HEAD_DIM)
    f_logit = p[..., _OFF_F:_OFF_G]
    g_conv = jax.nn.sigmoid(p[..., _OFF_G:_OFF_G + D_MODEL])
    g_attn = jax.nn.sigmoid(p[..., _OFF_G + D_MODEL:])
    y_conv = conv_branch(p[..., :_OFF_Q], conv_w, conv_b, conv_ln_g, conv_ln_b,
                         w_conv_out, b_conv_out)
    y_attn = forgetting_attention(q, k, v, f_logit) @ w_attn_out
    return (g_conv * y_conv + g_attn * y_attn) @ w_o


def swiglu(x, wg, wu, wd):
    return (jax.nn.silu(x @ wg) * (x @ wu)) @ wd


def moe_swiglu(x, w_router, b_router, wg, wu, wd):
    B, S, D = x.shape
    t = x.reshape(-1, D)
    logits = (t @ w_router).astype(jnp.float32) + b_router.astype(jnp.float32)
    top_v, top_i = lax.top_k(logits, TOP_K)
    top_w = jax.nn.softmax(top_v, axis=-1)
    comb = jnp.sum(jax.nn.one_hot(top_i, N_EXPERTS, dtype=jnp.float32) * top_w[..., None], axis=1)
    comb = comb.astype(x.dtype)
    y = jnp.zeros_like(t)
    for e in range(N_EXPERTS):
        y = y + comb[:, e:e + 1] * swiglu(t, wg[e], wu[e], wd[e])
    return y.reshape(B, S, D)


def _normal(key, shape, scale):
    return jax.random.normal(key, shape, jnp.float32) * scale


def setup_inputs(seed: int = 0) -> dict:
    key = jax.random.key(seed)
    ks = jax.random.split(key, 24)
    D, Cc, F, E = D_MODEL, CONV_WIDTH, D_FF, N_EXPERTS
    x = _normal(ks[0], (BATCH, SEQ, D), 1.0)
    w_in = _normal(ks[1], (DEPTH, D, IN_COLS), D ** -0.5)
    w_in = w_in.at[:, :, _OFF_V:_OFF_F].multiply(DEEPNORM_BETA)
    b_in = _normal(ks[2], (DEPTH, IN_COLS), 0.02).at[:, _OFF_F:_OFF_G].add(FORGET_BIAS_SHIFT)
    conv_w = _normal(ks[3], (DEPTH, CONV_KERNEL, Cc), CONV_KERNEL ** -0.5)
    conv_b = _normal(ks[4], (DEPTH, Cc), 0.02)
    conv_ln_g = 1.0 + _normal(ks[5], (DEPTH, Cc), 0.02)
    conv_ln_b = _normal(ks[6], (DEPTH, Cc), 0.02)
    w_conv_out = _normal(ks[7], (DEPTH, Cc, D), Cc ** -0.5)
    b_conv_out = _normal(ks[8], (DEPTH, D), 0.02)
    w_attn_out = _normal(ks[9], (DEPTH, ATTN_WIDTH, D), ATTN_WIDTH ** -0.5)
    w_o = _normal(ks[10], (DEPTH, D, D), DEEPNORM_BETA * D ** -0.5)
    ln1_g = 1.0 + _normal(ks[11], (DEPTH, D), 0.02)
    ln1_b = _normal(ks[12], (DEPTH, D), 0.02)
    ffn_wg = _normal(ks[13], (N_DENSE, D, F), D ** -0.5)
    ffn_wu = _normal(ks[14], (N_DENSE, D, F), D ** -0.5)
    ffn_wd = _normal(ks[15], (N_DENSE, F, D), DEEPNORM_BETA * F ** -0.5)
    router_w = _normal(ks[16], (N_MOE, D, E), D ** -0.5)
    router_b = _normal(ks[17], (N_MOE, E), 0.01)
    exp_wg = _normal(ks[18], (N_MOE, E, D, F), D ** -0.5)
    exp_wu = _normal(ks[19], (N_MOE, E, D, F), D ** -0.5)
    exp_wd = _normal(ks[20], (N_MOE, E, F, D), DEEPNORM_BETA * F ** -0.5)
    ln2_g = 1.0 + _normal(ks[21], (DEPTH, D), 0.02)
    ln2_b = _normal(ks[22], (DEPTH, D), 0.02)
    return {"x": x, "w_in": w_in, "b_in": b_in, "conv_w": conv_w, "conv_b": conv_b,
            "conv_ln_g": conv_ln_g, "conv_ln_b": conv_ln_b, "w_conv_out": w_conv_out,
            "b_conv_out": b_conv_out, "w_attn_out": w_attn_out, "w_o": w_o,
            "ln1_g": ln1_g, "ln1_b": ln1_b, "ffn_wg": ffn_wg, "ffn_wu": ffn_wu, "ffn_wd": ffn_wd,
            "router_w": router_w, "router_b": router_b, "exp_wg": exp_wg, "exp_wu": exp_wu,
            "exp_wd": exp_wd, "ln2_g": ln2_g, "ln2_b": ln2_b}


def reference(x, w_in, b_in, conv_w, conv_b, conv_ln_g, conv_ln_b, w_conv_out, b_conv_out,
              w_attn_out, w_o, ln1_g, ln1_b, ffn_wg, ffn_wu, ffn_wd, router_w, router_b,
              exp_wg, exp_wu, exp_wd, ln2_g, ln2_b):
    for l in range(DEPTH):
        h = mixer(x, w_in[l], b_in[l], conv_w[l], conv_b[l], conv_ln_g[l], conv_ln_b[l],
                  w_conv_out[l], b_conv_out[l], w_attn_out[l], w_o[l])
        x = layer_norm(DEEPNORM_ALPHA * x + h, ln1_g[l], ln1_b[l])
        j = l // 2
        if l % 2 == 0:
            h = swiglu(x, ffn_wg[j], ffn_wu[j], ffn_wd[j])
        else:
            h = moe_swiglu(x, router_w[j], router_b[j], exp_wg[j], exp_wu[j], exp_wd[j])
        x = layer_norm(DEEPNORM_ALPHA * x + h, ln2_g[l], ln2_b[l])
    return x
```

```python
import functools

import jax
import jax.numpy as jnp
from jax import lax
from jax.experimental import pallas as pl
from jax.experimental.pallas import tpu as pltpu

D_MODEL = 2048
BATCH = 4
SEQ = 2048
DEPTH = 4
N_HEADS = 8
HEAD_DIM = 128
ATTN_WIDTH = N_HEADS * HEAD_DIM
CONV_WIDTH = D_MODEL // 2
CONV_KERNEL = 31
D_FF = 5632
N_EXPERTS = 8
TOP_K = 2
LN_EPS = 1e-5
DEEPNORM_ALPHA = (2 * DEPTH) ** 0.25
TOKENS = BATCH * SEQ

OFF_Q = 2 * CONV_WIDTH
OFF_K = OFF_Q + ATTN_WIDTH
OFF_V = OFF_K + ATTN_WIDTH
OFF_F = OFF_V + ATTN_WIDTH
OFF_G = OFF_F + N_HEADS
IN_COLS = OFF_G + 2 * D_MODEL

LANES = 128
VMEM_LIMIT = 58 * 1024 * 1024

F32 = jnp.float32
BF16 = jnp.bfloat16

TM = 512
TM_MOE = 512
ROWS_MOE = TOKENS * TOP_K + N_EXPERTS * TM_MOE


def _params(n_axes):
    return pltpu.CompilerParams(
        dimension_semantics=("arbitrary",) * n_axes, vmem_limit_bytes=VMEM_LIMIT)


def _layer_norm(z, g, b):
    mu = jnp.mean(z, axis=-1, keepdims=True)
    zc = z - mu
    var = jnp.mean(zc * zc, axis=-1, keepdims=True)
    return zc * lax.rsqrt(var + LN_EPS) * g + b


def _mm(name, lhs, ws, consts, extras, outs, epilogue, *, tm, tn, nj,
        tile_e=None, nvalid=None):
    rows = lhs[0].shape[0]
    ni = rows // tm
    assert ni * tm == rows
    if tile_e is None:
        tile_e = jnp.zeros((ni,), jnp.int32)
        nvalid = jnp.full((1,), ni, jnp.int32)
    n_l, n_w, n_c, n_x, n_o = len(lhs), len(ws), len(consts), len(extras), len(outs)
    cast_idx = [p for p, (w, _, _) in enumerate(ws) if w.dtype != BF16]

    def body(te_ref, nv_ref, *refs):
        l_refs = refs[:n_l]
        w_refs = refs[n_l:n_l + n_w]
        c_refs = refs[n_l + n_w:n_l + n_w + n_c]
        x_refs = refs[n_l + n_w + n_c:n_l + n_w + n_c + n_x]
        o_refs = refs[n_l + n_w + n_c + n_x:n_l + n_w + n_c + n_x + n_o]
        scratch = refs[n_l + n_w + n_c + n_x + n_o:]
        j = pl.program_id(0)
        i = pl.program_id(1)
        fresh = (i == 0) | (te_ref[i] != te_ref[jnp.maximum(i - 1, 0)])

        if cast_idx:
            @pl.when(fresh)
            def _():
                for s_ref, p in zip(scratch, cast_idx):
                    k_rows = s_ref.shape[0]
                    ck = 256

                    def cast_chunk(c, _, s_ref=s_ref, w_ref=w_refs[p]):
                        r0 = pl.multiple_of(c * ck, ck)
                        s_ref[pl.ds(r0, ck), :] = w_ref[pl.ds(r0, ck), :].astype(BF16)
                        return 0

                    lax.fori_loop(0, k_rows // ck, cast_chunk, 0)

        @pl.when(i < nv_ref[0])
        def _():
            accs = []
            for p, (w, li, _) in enumerate(ws):
                wv = scratch[cast_idx.index(p)][...] if p in cast_idx else w_refs[p][...]
                accs.append(jnp.dot(l_refs[li][...], wv, preferred_element_type=F32))
            res = epilogue(j, accs, [c[...] for c in c_refs], [x[...] for x in x_refs])
            for o, r in zip(o_refs, res):
                o[...] = r.astype(o.dtype)

        @pl.when(i >= nv_ref[0])
        def _():
            for o in o_refs:
                o[...] = jnp.zeros(o.shape, o.dtype)

    in_specs = []
    for a in lhs:
        in_specs.append(pl.BlockSpec((tm, a.shape[1]), lambda j, i, te, nv: (i, 0)))
    for w, _, fn in ws:
        in_specs.append(pl.BlockSpec(
            (None, w.shape[1], tn),
            lambda j, i, te, nv, fn=fn: (fn(j, te[i])[0], 0, fn(j, te[i])[1])))
    for c, fn in consts:
        in_specs.append(pl.BlockSpec(
            (None, 1, c.shape[2] if c.shape[2] < tn else tn),
            lambda j, i, te, nv, fn=fn: (fn(j, te[i])[0], 0, fn(j, te[i])[1])))
    for a, width, fn in extras:
        in_specs.append(pl.BlockSpec((tm, width), lambda j, i, te, nv, fn=fn: (i, fn(j))))
    out_specs = [pl.BlockSpec((tm, width), lambda j, i, te, nv, fn=fn: (i, fn(j)))
                 for _, _, width, fn in outs]
    out_shape = [jax.ShapeDtypeStruct((rows, n), dt) for n, dt, _, _ in outs]
    scratch_shapes = [pltpu.VMEM((ws[p][0].shape[1], tn), BF16) for p in cast_idx]

    res = pl.pallas_call(
        body,
        grid_spec=pltpu.PrefetchScalarGridSpec(
            num_scalar_prefetch=2, grid=(nj, ni), in_specs=in_specs, out_specs=out_specs,
            scratch_shapes=scratch_shapes),
        out_shape=out_shape,
        compiler_params=_params(2),
        name=name,
    )(tile_e, nvalid, *lhs, *[w for w, _, _ in ws], *[c for c, _ in consts],
      *[a for a, _, _ in extras])
    return res


TS_F = 256


def _fcum_body(x_ref, wf_ref, bf_ref, o_ref, carry):
    s = pl.program_id(1)

    @pl.when(s == 0)
    def _():
        carry[...] = jnp.zeros(carry.shape, F32)

    f = jnp.dot(x_ref[...], wf_ref[...], precision=lax.Precision.HIGHEST,
                preferred_element_type=F32) + bf_ref[...]
    ls = jnp.minimum(f, 0.0) - jnp.log1p(jnp.exp(-jnp.abs(f)))
    r = lax.broadcasted_iota(jnp.int32, (TS_F, TS_F), 0)
    c = lax.broadcasted_iota(jnp.int32, (TS_F, TS_F), 1)
    tri = (c <= r).astype(F32)
    cum = jnp.dot(tri, ls, precision=lax.Precision.HIGHEST,
                  preferred_element_type=F32) + carry[...]
    o_ref[...] = cum
    carry[...] = cum[TS_F - 1:TS_F, :]


def _forget_cumsum(x3, wf, bf):
    return pl.pallas_call(
        _fcum_body,
        grid=(BATCH, SEQ // TS_F),
        in_specs=[pl.BlockSpec((None, TS_F, D_MODEL), lambda b, s: (b, s, 0)),
                  pl.BlockSpec((D_MODEL, LANES), lambda b, s: (0, 0)),
                  pl.BlockSpec((1, LANES), lambda b, s: (0, 0))],
        out_specs=pl.BlockSpec((None, TS_F, LANES), lambda b, s: (b, s, 0)),
        out_shape=jax.ShapeDtypeStruct((BATCH, SEQ, LANES), F32),
        scratch_shapes=[pltpu.VMEM((1, LANES), F32)],
        compiler_params=_params(2),
        name="forget_cumsum",
    )(x3, wf, bf)


TS_C = 128
HALO = 32


def _conv_body(prev_ref, cur_ref, w_ref, b_ref, g_ref, beta_ref, o_ref, win, ybuf):
    s = pl.program_id(1)
    keep = (s > 0).astype(F32)
    win[0:HALO, :] = prev_ref[...] * keep
    win[HALO:, :] = cur_ref[...]
    base = HALO - (CONV_KERNEL - 1)

    def chan(c, _):
        c0 = pl.multiple_of(c * LANES, LANES)
        acc = jnp.zeros((TS_C, LANES), F32) + b_ref[:, pl.ds(c0, LANES)]
        for k in range(CONV_KERNEL):
            acc = acc + w_ref[k:k + 1, pl.ds(c0, LANES)] * win[base + k:base + k + TS_C, pl.ds(c0, LANES)]
        ybuf[:, pl.ds(c0, LANES)] = acc
        return 0

    lax.fori_loop(0, CONV_WIDTH // LANES, chan, 0)
    y = _layer_norm(ybuf[...], g_ref[...], beta_ref[...])
    o_ref[...] = (y * jax.nn.sigmoid(y)).astype(o_ref.dtype)


def _conv_module(u3, w, b, g, beta):
    nh = TS_C // HALO
    return pl.pallas_call(
        _conv_body,
        grid=(BATCH, SEQ // TS_C),
        in_specs=[pl.BlockSpec((None, HALO, CONV_WIDTH),
                               lambda bi, s: (bi, jnp.maximum(s * nh - 1, 0), 0)),
                  pl.BlockSpec((None, TS_C, CONV_WIDTH), lambda bi, s: (bi, s, 0)),
                  pl.BlockSpec((CONV_KERNEL, CONV_WIDTH), lambda bi, s: (0, 0)),
                  pl.BlockSpec((1, CONV_WIDTH), lambda bi, s: (0, 0)),
                  pl.BlockSpec((1, CONV_WIDTH), lambda bi, s: (0, 0)),
                  pl.BlockSpec((1, CONV_WIDTH), lambda bi, s: (0, 0))],
        out_specs=pl.BlockSpec((None, TS_C, CONV_WIDTH), lambda bi, s: (bi, s, 0)),
        out_shape=jax.ShapeDtypeStruct((BATCH, SEQ, CONV_WIDTH), BF16),
        scratch_shapes=[pltpu.VMEM((HALO + TS_C, CONV_WIDTH), F32),
                        pltpu.VMEM((TS_C, CONV_WIDTH), F32)],
        compiler_params=_params(2),
        name="conv_module",
    )(u3, u3, w, b, g, beta)


TQ = 256
NQ = SEQ // TQ


def _attn_body(q_ref, k_ref, v_ref, cq_ref, ck_ref, o_ref):
    h = pl.program_id(1)
    i = pl.program_id(2)
    q = q_ref[...]
    lane = lax.broadcasted_iota(jnp.int32, (TQ, LANES), 1)
    cq = jnp.sum(jnp.where(lane == h, cq_ref[...], 0.0), axis=1, keepdims=True)

    def scores(kb):
        k0 = pl.multiple_of(kb * TQ, TQ)
        k = k_ref[pl.ds(k0, TQ), :]
        s = lax.dot_general(q, k, (((1,), (1,)), ((), ())), preferred_element_type=F32)
        return s + (cq - ck_ref[:, pl.ds(k0, TQ)]), k0

    def update(carry, s, k0):
        m, l, acc = carry
        m_new = jnp.maximum(m, jnp.max(s, axis=1, keepdims=True))
        a = jnp.exp(m - m_new)
        p = jnp.exp(s - m_new)
        l = a * l + jnp.sum(p, axis=1, keepdims=True)
        v = v_ref[pl.ds(k0, TQ), :]
        acc = a * acc + jnp.dot(p.astype(BF16), v, preferred_element_type=F32)
        return m_new, l, acc

    def off_diag(kb, carry):
        s, k0 = scores(kb)
        return update(carry, s, k0)

    init = (jnp.full((TQ, 1), -jnp.inf, F32), jnp.zeros((TQ, 1), F32),
            jnp.zeros((TQ, HEAD_DIM), F32))
    carry = lax.fori_loop(0, i, off_diag, init)
    s, k0 = scores(i)
    r = lax.broadcasted_iota(jnp.int32, (TQ, TQ), 0)
    c = lax.broadcasted_iota(jnp.int32, (TQ, TQ), 1)
    s = jnp.where(c <= r, s, -jnp.inf)
    m, l, acc = update(carry, s, k0)
    o_ref[...] = (acc / l).astype(o_ref.dtype)


def _attention(qkv, cum, cum_t):
    return pl.pallas_call(
        _attn_body,
        grid=(BATCH, N_HEADS, NQ),
        in_specs=[pl.BlockSpec((TQ, HEAD_DIM), lambda b, h, i: (b * NQ + i, h)),
                  pl.BlockSpec((SEQ, HEAD_DIM), lambda b, h, i: (b, N_HEADS + h)),
                  pl.BlockSpec((SEQ, HEAD_DIM), lambda b, h, i: (b, 2 * N_HEADS + h)),
                  pl.BlockSpec((None, TQ, LANES), lambda b, h, i: (b, i, 0)),
                  pl.BlockSpec((None, None, 1, SEQ), lambda b, h, i: (b, h, 0, 0))],
        out_specs=pl.BlockSpec((TQ, HEAD_DIM), lambda b, h, i: (b * NQ + i, h)),
        out_shape=jax.ShapeDtypeStruct((TOKENS, ATTN_WIDTH), BF16),
        compiler_params=_params(3),
        name="fox_attention",
    )(qkv, qkv, qkv, cum, cum_t)


TL = 256


def _res_ln_body(x_ref, y_ref, g_ref, b_ref, o_ref, ob_ref):
    z = DEEPNORM_ALPHA * x_ref[...] + y_ref[...]
    y = _layer_norm(z, g_ref[...], b_ref[...])
    o_ref[...] = y
    ob_ref[...] = y.astype(BF16)


def _res_ln(x, y, g, b):
    row = pl.BlockSpec((TL, D_MODEL), lambda i: (i, 0))
    vec = pl.BlockSpec((1, D_MODEL), lambda i: (0, 0))
    return pl.pallas_call(
        _res_ln_body,
        grid=(TOKENS // TL,),
        in_specs=[row, row, vec, vec],
        out_specs=[row, row],
        out_shape=[jax.ShapeDtypeStruct((TOKENS, D_MODEL), F32),
                   jax.ShapeDtypeStruct((TOKENS, D_MODEL), BF16)],
        compiler_params=_params(1),
        name="residual_ln",
    )(x, y, g, b)


TR = 512


def _router_body(x_ref, w_ref, b_ref, idx_ref, wgt_ref):
    logits = jnp.dot(x_ref[...], w_ref[...], precision=lax.Precision.HIGHEST,
                     preferred_element_type=F32) + b_ref[...]
    lane = lax.broadcasted_iota(jnp.int32, logits.shape, 1)
    valid = lane < N_EXPERTS
    logits = jnp.where(valid, logits, -jnp.inf)
    m1 = jnp.max(logits, axis=1, keepdims=True)
    i1 = jnp.min(jnp.where(logits == m1, lane, LANES), axis=1, keepdims=True)
    rest = jnp.where(lane == i1, -jnp.inf, logits)
    m2 = jnp.max(rest, axis=1, keepdims=True)
    i2 = jnp.min(jnp.where(rest == m2, lane, LANES), axis=1, keepdims=True)
    e2 = jnp.exp(m2 - m1)
    den = 1.0 + e2
    col = lax.broadcasted_iota(jnp.int32, (TR, TOP_K), 1)
    idx_ref[...] = jnp.where(col == 0, i1, i2)
    wgt_ref[...] = jnp.where(col == 0, 1.0 / den, e2 / den)


def _router(x, w, b):
    return pl.pallas_call(
        _router_body,
        grid=(TOKENS // TR,),
        in_specs=[pl.BlockSpec((TR, D_MODEL), lambda i: (i, 0)),
                  pl.BlockSpec((D_MODEL, LANES), lambda i: (0, 0)),
                  pl.BlockSpec((1, LANES), lambda i: (0, 0))],
        out_specs=[pl.BlockSpec((TR, TOP_K), lambda i: (i, 0)),
                   pl.BlockSpec((TR, TOP_K), lambda i: (i, 0))],
        out_shape=[jax.ShapeDtypeStruct((TOKENS, TOP_K), jnp.int32),
                   jax.ShapeDtypeStruct((TOKENS, TOP_K), F32)],
        compiler_params=_params(1),
        name="router",
    )(x, w, b)


TG = 256


def _row_copy(src_hbm, dst, sem, src_row, dst_row):
    return pltpu.make_async_copy(src_hbm.at[pl.ds(src_row, 1), :], dst.at[pl.ds(dst_row, 1), :], sem)


def _gather_body(tok_ref, x_hbm, o_ref, buf, sem):
    base = pl.program_id(0) * TG

    def start(r, _):
        _row_copy(x_hbm, buf, sem, tok_ref[base + r], r).start()
        return 0

    lax.fori_loop(0, TG, start, 0)

    def wait(r, _):
        _row_copy(x_hbm, buf, sem, tok_ref[base + r], r).wait()
        return 0

    lax.fori_loop(0, TG, wait, 0)
    o_ref[...] = buf[...].astype(o_ref.dtype)


def _gather_rows(src_tok, x):
    return pl.pallas_call(
        _gather_body,
        grid_spec=pltpu.PrefetchScalarGridSpec(
            num_scalar_prefetch=1, grid=(ROWS_MOE // TG,),
            in_specs=[pl.BlockSpec(memory_space=pl.ANY)],
            out_specs=pl.BlockSpec((TG, D_MODEL), lambda i, tok: (i, 0)),
            scratch_shapes=[pltpu.VMEM((TG, D_MODEL), F32), pltpu.SemaphoreType.DMA(())]),
        out_shape=jax.ShapeDtypeStruct((ROWS_MOE, D_MODEL), BF16),
        compiler_params=_params(1),
        name="moe_gather",
    )(src_tok, x)


TC = 128


def _combine_body(pos_ref, x_ref, y_hbm, g_ref, b_ref, o_ref, ob_ref, buf, sem):
    base = pl.program_id(0) * TC

    def start(t, _):
        for k in range(TOP_K):
            _row_copy(y_hbm, buf.at[k], sem.at[k], pos_ref[(base + t) * TOP_K + k], t).start()
        return 0

    lax.fori_loop(0, TC, start, 0)

    def wait(t, _):
        for k in range(TOP_K):
            _row_copy(y_hbm, buf.at[k], sem.at[k], pos_ref[(base + t) * TOP_K + k], t).wait()
        return 0

    lax.fori_loop(0, TC, wait, 0)
    z = DEEPNORM_ALPHA * x_ref[...] + (buf[0] + buf[1])
    y = _layer_norm(z, g_ref[...], b_ref[...])
    o_ref[...] = y
    ob_ref[...] = y.astype(BF16)


def _combine_ln(pos, x, yrows, g, b):
    row = pl.BlockSpec((TC, D_MODEL), lambda i, pos: (i, 0))
    vec = pl.BlockSpec((1, D_MODEL), lambda i, pos: (0, 0))
    return pl.pallas_call(
        _combine_body,
        grid_spec=pltpu.PrefetchScalarGridSpec(
            num_scalar_prefetch=1, grid=(TOKENS // TC,),
            in_specs=[row, pl.BlockSpec(memory_space=pl.ANY), vec, vec],
            out_specs=[row, row],
            scratch_shapes=[pltpu.VMEM((TOP_K, TC, D_MODEL), F32),
                            pltpu.SemaphoreType.DMA((TOP_K,))]),
        out_shape=[jax.ShapeDtypeStruct((TOKENS, D_MODEL), F32),
                   jax.ShapeDtypeStruct((TOKENS, D_MODEL), BF16)],
        compiler_params=_params(1),
        name="moe_combine_ln",
    )(pos, x, yrows, g, b)


def _mixer(l, x, xb, w_in, b_in3, w_gate, b_gate3, wf, bf, conv_w, conv_b, conv_ln_g, conv_ln_b,
           w_conv_out, b_conv_out3, w_attn_out, w_o, ln_g, ln_b):
    tn = 512
    nb = CONV_WIDTH // tn
    (u,) = _mm(
        "glu_proj", [xb],
        [(w_in, 0, lambda j, e: (l, j)), (w_in, 0, lambda j, e: (l, nb + j))],
        [(b_in3, lambda j, e: (l, j)), (b_in3, lambda j, e: (l, nb + j))],
        [], [(CONV_WIDTH, F32, tn, lambda j: j)],
        lambda j, a, c, x_: [(a[0] + c[0]) * jax.nn.sigmoid(a[1] + c[1])],
        tm=TM, tn=tn, nj=nb)
    qb = OFF_Q // tn
    nqb = ATTN_WIDTH // tn
    (qkv,) = _mm(
        "qkv_proj", [xb], [(w_in, 0, lambda j, e: (l, qb + j))],
        [(b_in3, lambda j, e: (l, qb + j))], [],
        [(3 * ATTN_WIDTH, BF16, tn, lambda j: j)],
        lambda j, a, c, x_: [(a[0] + c[0]) * jnp.where(j < nqb, HEAD_DIM ** -0.5, 1.0)],
        tm=TM, tn=tn, nj=3 * nqb)
    (gates,) = _mm(
        "gate_proj", [xb], [(w_gate, 0, lambda j, e: (l, j))],
        [(b_gate3, lambda j, e: (l, j))], [],
        [(2 * D_MODEL, BF16, tn, lambda j: j)],
        lambda j, a, c, x_: [jax.nn.sigmoid(a[0] + c[0])],
        tm=TM, tn=tn, nj=2 * D_MODEL // tn)
    cum = _forget_cumsum(x.reshape(BATCH, SEQ, D_MODEL), wf[l], bf[l])
    cum_t = jnp.transpose(cum[:, :, :N_HEADS], (0, 2, 1)).reshape(BATCH, N_HEADS, 1, SEQ)
    y_attn = _attention(qkv, cum, cum_t)
    y_conv = _conv_module(u.reshape(BATCH, SEQ, CONV_WIDTH), conv_w[l], conv_b[l][None],
                          conv_ln_g[l][None], conv_ln_b[l][None]).reshape(TOKENS, CONV_WIDTH)
    gb = D_MODEL // tn
    (merged,) = _mm(
        "branch_merge", [y_conv, y_attn],
        [(w_conv_out, 0, lambda j, e: (l, j)), (w_attn_out, 1, lambda j, e: (l, j))],
        [(b_conv_out3, lambda j, e: (l, j))],
        [(gates, tn, lambda j: j), (gates, tn, lambda j: gb + j)],
        [(D_MODEL, BF16, tn, lambda j: j)],
        lambda j, a, c, x_: [x_[0].astype(F32) * (a[0] + c[0]) + x_[1].astype(F32) * a[1]],
        tm=TM, tn=tn, nj=gb)
    tn2 = 1024
    (h,) = _mm(
        "out_proj", [merged], [(w_o, 0, lambda j, e: (l, j))], [], [],
        [(D_MODEL, F32, tn2, lambda j: j)],
        lambda j, a, c, x_: [a[0]],
        tm=TM, tn=tn2, nj=D_MODEL // tn2)
    return _res_ln(x, h, ln_g[l][None], ln_b[l][None])


def _ffn_up(name, xs, wg, wu, g_of, tm, tile_e=None, nvalid=None):
    tn = 512
    (h,) = _mm(
        name, [xs], [(wg, 0, lambda j, e: (g_of(e), j)), (wu, 0, lambda j, e: (g_of(e), j))],
        [], [], [(D_FF, BF16, tn, lambda j: j)],
        lambda j, a, c, x_: [a[0] * jax.nn.sigmoid(a[0]) * a[1]],
        tm=tm, tn=tn, nj=D_FF // tn, tile_e=tile_e, nvalid=nvalid)
    return h


def _ffn_down(name, h, wd, g_of, tm, roww=None, tile_e=None, nvalid=None):
    tn = 512
    extras = [] if roww is None else [(roww, LANES, lambda j: 0)]
    if roww is None:
        epi = lambda j, a, c, x_: [a[0]]
    else:
        epi = lambda j, a, c, x_: [a[0] * x_[0][:, 0:1]]
    (y,) = _mm(
        name, [h], [(wd, 0, lambda j, e: (g_of(e), j))], [], extras,
        [(D_MODEL, F32, tn, lambda j: j)], epi,
        tm=tm, tn=tn, nj=D_MODEL // tn, tile_e=tile_e, nvalid=nvalid)
    return y


def _route(top_i, top_w):
    n_tiles = ROWS_MOE // TM_MOE
    flat_e = top_i.reshape(-1)
    onehot = (flat_e[:, None] == jnp.arange(N_EXPERTS, dtype=jnp.int32)[None, :]).astype(jnp.int32)
    csum = jnp.cumsum(onehot, axis=0)
    rank = jnp.sum((csum - onehot) * onehot, axis=1)
    counts = csum[-1]
    tiles = (counts + TM_MOE - 1) // TM_MOE
    tile_end = jnp.cumsum(tiles)
    starts = (tile_end - tiles) * TM_MOE
    dest = starts[flat_e] + rank
    src_tok = jnp.zeros((ROWS_MOE,), jnp.int32).at[dest].set(
        jnp.arange(TOKENS * TOP_K, dtype=jnp.int32) // TOP_K)
    roww = jnp.zeros((ROWS_MOE,), F32).at[dest].set(top_w.reshape(-1))
    tile_e = jnp.minimum(
        jnp.searchsorted(tile_end, jnp.arange(n_tiles, dtype=jnp.int32), side="right"),
        N_EXPERTS - 1).astype(jnp.int32)
    nvalid = tile_end[-1:].astype(jnp.int32)
    return src_tok, jnp.broadcast_to(roww[:, None], (ROWS_MOE, LANES)), tile_e, nvalid, dest.astype(jnp.int32)


def kernel(x, w_in, b_in, conv_w, conv_b, conv_ln_g, conv_ln_b, w_conv_out, b_conv_out, w_attn_out, w_o, ln1_g, ln1_b, ffn_wg, ffn_wu, ffn_wd, router_w, router_b, exp_wg, exp_wu, exp_wd, ln2_g, ln2_b):
    x = x.reshape(TOKENS, D_MODEL)
    xb = x.astype(BF16)
    b_in3 = b_in[:, None, :]
    w_gate = w_in[:, :, OFF_G:]
    b_gate3 = b_in[:, None, OFF_G:]
    wf = jnp.pad(w_in[:, :, OFF_F:OFF_G], ((0, 0), (0, 0), (0, LANES - N_HEADS)))
    bf = jnp.pad(b_in[:, None, OFF_F:OFF_G], ((0, 0), (0, 0), (0, LANES - N_HEADS)))
    b_conv_out3 = b_conv_out[:, None, :]
    wr = jnp.pad(router_w, ((0, 0), (0, 0), (0, LANES - N_EXPERTS)))
    br = jnp.pad(router_b[:, None, :], ((0, 0), (0, 0), (0, LANES - N_EXPERTS)))
    n_moe = exp_wg.shape[0]
    ewg = exp_wg.reshape(n_moe * N_EXPERTS, D_MODEL, D_FF)
    ewu = exp_wu.reshape(n_moe * N_EXPERTS, D_MODEL, D_FF)
    ewd = exp_wd.reshape(n_moe * N_EXPERTS, D_FF, D_MODEL)

    for l in range(DEPTH):
        x, xb = _mixer(l, x, xb, w_in, b_in3, w_gate, b_gate3, wf, bf, conv_w, conv_b, conv_ln_g,
                       conv_ln_b, w_conv_out, b_conv_out3, w_attn_out, w_o, ln1_g, ln1_b)
        jj = l // 2
        if l % 2 == 0:
            h = _ffn_up("ffn_up", xb, ffn_wg, ffn_wu, lambda e: jj, TM)
            y = _ffn_down("ffn_down", h, ffn_wd, lambda e: jj, TM)
            x, xb = _res_ln(x, y, ln2_g[l][None], ln2_b[l][None])
        else:
            top_i, top_w = _router(x, wr[jj], br[jj])
            src_tok, roww, tile_e, nvalid, pos = _route(top_i, top_w)
            xs = _gather_rows(src_tok, x)
            g_of = lambda e: jj * N_EXPERTS + e
            h = _ffn_up("moe_up", xs, ewg, ewu, g_of, TM_MOE, tile_e, nvalid)
            yrows = _ffn_down("moe_down", h, ewd, g_of, TM_MOE, roww, tile_e, nvalid)
            x, xb = _combine_ln(pos, x, yrows, ln2_g[l][None], ln2_b[l][None])
    return x.reshape(BATCH, SEQ, D_MODEL)
```
